```python
import math
import jax, jax.numpy as jnp
from jax import lax
import numpy as np

D_MODEL = 2048
BATCH = 4
SEQ = 4096
DEPTH = 1

CTX_LEN = 256
GRID_W = 64
Q_BLOCK = 128
ROPE_THETA = 10000.0
EPS = 1e-6
ADA_CHUNKS = 6

A_HEADS = 16
A_KV_HEADS = 4
A_HEAD_DIM = 128
B_HEADS = 8
B_HEAD_DIM = 64
B_V_DIM = 2 * B_HEAD_DIM

A_Q = A_HEADS * A_HEAD_DIM
A_KV = A_KV_HEADS * A_HEAD_DIM
B_QK = B_HEADS * 2 * B_HEAD_DIM
B_V = B_HEADS * B_V_DIM
KV_WIDTH = 2 * A_KV + B_QK + B_V
IN_WIDTH = KV_WIDTH + A_Q + B_QK + 2 * D_MODEL

D_FF = -(-8 * D_MODEL // (3 * 256)) * 256

kernel_name = "hybrid_gqa_diffattn_prefix_dit_block"


def rms_norm(x, g):
    xf = x.astype(jnp.float32)
    y = xf * lax.rsqrt(jnp.mean(xf * xf, axis=-1, keepdims=True) + EPS)
    return (y * g.astype(jnp.float32)).astype(x.dtype)


def modulate(h, shift, scale):
    return h * (1 + scale) + shift


def adaln(cond, w_ada, b_ada):
    return jnp.split(jax.nn.silu(cond) @ w_ada + b_ada, ADA_CHUNKS, axis=-1)


def axial_rope(rows, head_dim):
    n_freq = head_dim // 4
    freqs = ROPE_THETA ** (-jnp.arange(n_freq, dtype=jnp.float32) / n_freq)
    row = jnp.repeat(jnp.arange(rows, dtype=jnp.float32), GRID_W)
    col = jnp.tile(jnp.arange(GRID_W, dtype=jnp.float32), rows)
    ang = jnp.concatenate([row[:, None] * freqs, col[:, None] * freqs], axis=-1)
    return jnp.cos(ang), jnp.sin(ang)


def apply_rope(x, cos, sin):
    half = x.shape[-1] // 2
    shape = (cos.shape[0],) + (1,) * (x.ndim - 3) + (half,)
    cos = cos.reshape(shape).astype(x.dtype)
    sin = sin.reshape(shape).astype(x.dtype)
    x1, x2 = x[..., :half], x[..., half:]
    return jnp.concatenate([x1 * cos - x2 * sin, x2 * cos + x1 * sin], axis=-1)


def split_kv(p_kv, k_norm_a, k_norm_b):
    b, n = p_kv.shape[:2]
    k_a, v_a, k_b, v_b = jnp.split(p_kv, [A_KV, 2 * A_KV, 2 * A_KV + B_QK], axis=-1)
    k_a = rms_norm(k_a.reshape(b, n, A_KV_HEADS, A_HEAD_DIM), k_norm_a)
    v_a = v_a.reshape(b, n, A_KV_HEADS, A_HEAD_DIM)
    k_b = rms_norm(k_b.reshape(b, n, B_HEADS, 2, B_HEAD_DIM), k_norm_b)
    v_b = v_b.reshape(b, n, B_HEADS, B_V_DIM)
    return k_a, v_a, k_b, v_b


def split_qg(p_qg, q_norm_a, q_norm_b):
    b, n = p_qg.shape[:2]
    q_a, q_b, g_a, g_b = jnp.split(p_qg, [A_Q, A_Q + B_QK, A_Q + B_QK + D_MODEL], axis=-1)
    q_a = rms_norm(q_a.reshape(b, n, A_HEADS, A_HEAD_DIM), q_norm_a)
    q_b = rms_norm(q_b.reshape(b, n, B_HEADS, 2, B_HEAD_DIM), q_norm_b)
    return q_a, q_b, g_a, g_b


def gqa(q, k, v):
    b, n, hq, d = q.shape
    hkv = k.shape[2]
    qg = q.reshape(b, n, hkv, hq // hkv, d)
    s = jnp.einsum('bnkgd,bmkd->bkgnm', qg, k).astype(jnp.float32) * (d ** -0.5)
    p = jax.nn.softmax(s, axis=-1).astype(v.dtype)
    o = jnp.einsum('bkgnm,bmkd->bnkgd', p, v)
    return o.reshape(b, n, hq * d)


def diff_attn(q, k, v, lam, lam_init, subln_g):
    b, n, h, _, dh = q.shape
    s = jnp.einsum('bnhid,bmhid->bhinm', q, k).astype(jnp.float32) * (dh ** -0.5)
    p = jax.nn.softmax(s, axis=-1)
    a = (p[:, :, 0] - lam * p[:, :, 1]).astype(v.dtype)
    o = jnp.einsum('bhnm,bmhe->bnhe', a, v)
    o = rms_norm(o, subln_g) * (1 - lam_init)
    return o.reshape(b, n, h * B_V_DIM)


def sweep_query_blocks(attn_fn, q):
    b, s = q.shape[:2]
    nblk = s // Q_BLOCK
    qb = jnp.moveaxis(q.reshape((b, nblk, Q_BLOCK) + q.shape[2:]), 1, 0)
    o = lax.map(attn_fn, qb)
    return jnp.moveaxis(o, 0, 1).reshape(b, s, o.shape[-1])


def merge_branches(o_a, o_b, g_a, g_b, w_br_a, w_br_b, w_out):
    merged = jax.nn.sigmoid(g_a) * (o_a @ w_br_a) + jax.nn.sigmoid(g_b) * (o_b @ w_br_b)
    return merged @ w_out


def swiglu_sublayer(x, shift, scale, gate, norm_g, w_ff_gate, w_ff_up, w_ff_down):
    h = modulate(rms_norm(x, norm_g), shift, scale)
    return x + gate * ((jax.nn.silu(h @ w_ff_gate) * (h @ w_ff_up)) @ w_ff_down)


def setup_inputs(seed: int = 0) -> dict:
    key = jax.random.key(seed)
    ks = jax.random.split(key, 24)
    f32 = jnp.float32
    L = DEPTH

    def w(k, shape, fan_in, s=1.0):
        return jax.random.normal(k, shape, f32) * (s * fan_in ** -0.5)

    def gain(k, shape):
        return 1.0 + 0.05 * jax.random.normal(k, shape, f32)

    return {
        'x': jax.random.normal(ks[0], (BATCH, SEQ, D_MODEL), f32),
        'c': jax.random.normal(ks[1], (BATCH, D_MODEL), f32),
        'ctx': jax.random.normal(ks[2], (BATCH, CTX_LEN, D_MODEL), f32),
        'c_ctx': jax.random.normal(ks[3], (D_MODEL,), f32),
        'w_ada': w(ks[4], (L, D_MODEL, ADA_CHUNKS * D_MODEL), D_MODEL, 0.5),
        'b_ada': 0.02 * jax.random.normal(ks[5], (L, ADA_CHUNKS * D_MODEL), f32),
        'norm1_g': gain(ks[6], (L, D_MODEL)),
        'w_in': w(ks[7], (L, D_MODEL, IN_WIDTH), D_MODEL),
        'q_norm_a': gain(ks[8], (L, A_HEAD_DIM)),
        'k_norm_a': gain(ks[9], (L, A_HEAD_DIM)),
        'q_norm_b': gain(ks[10], (L, B_HEAD_DIM)),
        'k_norm_b': gain(ks[11], (L, B_HEAD_DIM)),
        'lam_q1': 0.1 * jax.random.normal(ks[12], (L, B_HEAD_DIM), f32),
        'lam_k1': 0.1 * jax.random.normal(ks[13], (L, B_HEAD_DIM), f32),
        'lam_q2': 0.1 * jax.random.normal(ks[14], (L, B_HEAD_DIM), f32),
        'lam_k2': 0.1 * jax.random.normal(ks[15], (L, B_HEAD_DIM), f32),
        'subln_g': gain(ks[16], (L, B_V_DIM)),
        'w_br_a': w(ks[17], (L, A_Q, D_MODEL), A_Q),
        'w_br_b': w(ks[18], (L, B_V, D_MODEL), B_V),
        'w_out': w(ks[19], (L, D_MODEL, D_MODEL), D_MODEL),
        'norm2_g': gain(ks[20], (L, D_MODEL)),
        'w_ff_gate': w(ks[21], (L, D_MODEL, D_FF), D_MODEL),
        'w_ff_up': w(ks[22], (L, D_MODEL, D_FF), D_MODEL),
        'w_ff_down': w(ks[23], (L, D_FF, D_MODEL), D_FF),
    }


def reference(x, c, ctx, c_ctx, w_ada, b_ada, norm1_g, w_in, q_norm_a, k_norm_a, q_norm_b, k_norm_b,
              lam_q1, lam_k1, lam_q2, lam_k2, subln_g, w_br_a, w_br_b, w_out, norm2_g,
              w_ff_gate, w_ff_up, w_ff_down):
    rows = x.shape[1] // GRID_W
    cos_a, sin_a = axial_rope(rows, A_HEAD_DIM)
    cos_b, sin_b = axial_rope(rows, B_HEAD_DIM)

    for l in range(DEPTH):
        last = l == DEPTH - 1
        lam_init = 0.8 - 0.6 * math.exp(-0.3 * l)
        lam = (jnp.exp(jnp.sum(lam_q1[l].astype(jnp.float32) * lam_k1[l].astype(jnp.float32)))
               - jnp.exp(jnp.sum(lam_q2[l].astype(jnp.float32) * lam_k2[l].astype(jnp.float32)))
               + lam_init)
        m_lat = [m[:, None, :] for m in adaln(c, w_ada[l], b_ada[l])]
        m_ctx = adaln(c_ctx, w_ada[l], b_ada[l])
        w_in_l = w_in[l]

        hc = modulate(rms_norm(ctx, norm1_g[l]), m_ctx[0], m_ctx[1])
        pc = hc @ (w_in_l[:, :KV_WIDTH] if last else w_in_l)
        ka_c, va_c, kb_c, vb_c = split_kv(pc[..., :KV_WIDTH], k_norm_a[l], k_norm_b[l])

        h = modulate(rms_norm(x, norm1_g[l]), m_lat[0], m_lat[1])
        p = h @ w_in_l
        ka, va, kb, vb = split_kv(p[..., :KV_WIDTH], k_norm_a[l], k_norm_b[l])
        qa, qb, ga, gb = split_qg(p[..., KV_WIDTH:], q_norm_a[l], q_norm_b[l])
        qa = apply_rope(qa, cos_a, sin_a)
        ka = apply_rope(ka, cos_a, sin_a)
        qb = apply_rope(qb, cos_b, sin_b)
        kb = apply_rope(kb, cos_b, sin_b)

        ka_all = jnp.concatenate([ka_c, ka], axis=1)
        va_all = jnp.concatenate([va_c, va], axis=1)
        kb_all = jnp.concatenate([kb_c, kb], axis=1)
        vb_all = jnp.concatenate([vb_c, vb], axis=1)
        sg = subln_g[l]
        oa = sweep_query_blocks(lambda qblk: gqa(qblk, ka_all, va_all), qa)
        ob = sweep_query_blocks(lambda qblk: diff_attn(qblk, kb_all, vb_all, lam, lam_init, sg), qb)
        x_new = x + m_lat[2] * merge_branches(oa, ob, ga, gb, w_br_a[l], w_br_b[l], w_out[l])
        x_new = swiglu_sublayer(x_new, m_lat[3], m_lat[4], m_lat[5], norm2_g[l],
                                w_ff_gate[l], w_ff_up[l], w_ff_down[l])

        if not last:
            qa_c, qb_c, ga_c, gb_c = split_qg(pc[..., KV_WIDTH:], q_norm_a[l], q_norm_b[l])
            oa_c = gqa(qa_c, ka_c, va_c)
            ob_c = diff_attn(qb_c, kb_c, vb_c, lam, lam_init, sg)
            ctx = ctx + m_ctx[2] * merge_branches(oa_c, ob_c, ga_c, gb_c, w_br_a[l], w_br_b[l], w_out[l])
            ctx = swiglu_sublayer(ctx, m_ctx[3], m_ctx[4], m_ctx[5], norm2_g[l],
                                  w_ff_gate[l], w_ff_up[l], w_ff_down[l])
        x = x_new
    return x
```

```python
import functools
import math

import jax
import jax.numpy as jnp
from jax import lax
from jax.experimental import pallas as pl
from jax.experimental.pallas import tpu as pltpu

D_MODEL = 2048
GRID_W = 64
ROPE_THETA = 10000.0
EPS = 1e-6
ADA_CHUNKS = 6

A_HEADS = 16
A_KV_HEADS = 4
A_GROUP = A_HEADS // A_KV_HEADS
A_HEAD_DIM = 128
B_HEADS = 8
B_HEAD_DIM = 64
B_V_DIM = 2 * B_HEAD_DIM

A_Q = A_HEADS * A_HEAD_DIM
A_KV = A_KV_HEADS * A_HEAD_DIM
B_QK = B_HEADS * 2 * B_HEAD_DIM
B_V = B_HEADS * B_V_DIM
KV_WIDTH = 2 * A_KV + B_QK + B_V
IN_WIDTH = KV_WIDTH + A_Q + B_QK + 2 * D_MODEL

LANES = 128
LOG2E = math.log2(math.e)
LAM_INIT = 0.8 - 0.6 * math.exp(-0.3 * 0)

OFF_KA = 0
OFF_VA = A_KV
OFF_KB = 2 * A_KV
OFF_VB = 2 * A_KV + B_QK
OFF_QA = KV_WIDTH
OFF_QB = KV_WIDTH + A_Q
OFF_GA = KV_WIDTH + A_Q + B_QK
OFF_GB = OFF_GA + D_MODEL

VMEM_LIMIT = 48 * 1024 * 1024

BF16 = jnp.bfloat16
F32 = jnp.float32


def _silu(v):
    return v * (1.0 / (1.0 + jnp.exp(-v)))


def _sigmoid(v):
    return 1.0 / (1.0 + jnp.exp(-v))


def _split_bf16(v):
    hi = v.astype(BF16)
    lo = (v - hi.astype(F32)).astype(BF16)
    return hi, lo


def _adaln_kernel(cond_ref, w_ref, b_ref, o_ref):
    a = _silu(cond_ref[...])
    a_hi, a_lo = _split_bf16(a)
    w_hi, w_lo = _split_bf16(w_ref[...])
    acc = jnp.dot(a_hi, w_hi, preferred_element_type=F32)
    acc += jnp.dot(a_lo, w_hi, preferred_element_type=F32)
    acc += jnp.dot(a_hi, w_lo, preferred_element_type=F32)
    o_ref[...] = acc + b_ref[...]


def _adaln(cond, w_ada, b_ada):
    rows, d = cond.shape
    n = w_ada.shape[1]
    tn = 1024
    return pl.pallas_call(
        _adaln_kernel,
        grid=(n // tn,),
        in_specs=[
            pl.BlockSpec((rows, d), lambda j: (0, 0)),
            pl.BlockSpec((d, tn), lambda j: (0, j)),
            pl.BlockSpec((1, tn), lambda j: (0, j)),
        ],
        out_specs=pl.BlockSpec((rows, tn), lambda j: (0, j)),
        out_shape=jax.ShapeDtypeStruct((rows, n), F32),
        compiler_params=pltpu.CompilerParams(
            dimension_semantics=("arbitrary",), vmem_limit_bytes=VMEM_LIMIT),
        name="adaln",
    )(cond, w_ada, b_ada.reshape(1, n))


IN_TN = 512


def _rms_scale(y, width):
    sq = y * y
    tot = jnp.sum(sq, axis=-1, keepdims=True)
    if width == LANES:
        return lax.rsqrt(tot * (1.0 / LANES) + EPS)
    lane = lax.broadcasted_iota(jnp.int32, y.shape, 1)
    low = lane < width
    lo = jnp.sum(jnp.where(low, sq, 0.0), axis=-1, keepdims=True)
    hi = tot - lo
    return jnp.where(low, lax.rsqrt(lo * (1.0 / width) + EPS), lax.rsqrt(hi * (1.0 / width) + EPS))


def _rope_a(y, tabs):
    cos, sin = tabs
    return y * cos + pltpu.roll(y, 64, 1) * sin


def _rope_b(y, tabs):
    cos, sin_lo, sin_hi = tabs
    return y * cos + pltpu.roll(y, 96, 1) * sin_lo + pltpu.roll(y, 32, 1) * sin_hi


def _in_proj_kernel(x_ref, shift_ref, scale_ref, g1_ref, w_ref, gains_ref,
                    ca_ref, sa_ref, cb_ref, sb1_ref, sb2_ref, o_ref, h_ref, acc_ref,
                    *, use_rope, with_q):
    j = pl.program_id(1)

    @pl.when(j == 0)
    def _():
        xf = x_ref[...]
        inv = lax.rsqrt(jnp.mean(xf * xf, axis=-1, keepdims=True) + EPS)
        h = xf * inv * g1_ref[...]
        h = h * (1.0 + scale_ref[0]) + shift_ref[0]
        h_ref[...] = h.astype(BF16)

    acc_ref[...] = jnp.dot(h_ref[...], w_ref[...], preferred_element_type=F32)

    def head_epilogue(gain_row, width, rope, scale):
        gain = gains_ref[gain_row:gain_row + 1, :]
        for c in range(IN_TN // LANES):
            y = acc_ref[:, c * LANES:(c + 1) * LANES]
            y = y * _rms_scale(y, width) * gain
            if use_rope:
                if rope == "a":
                    y = _rope_a(y, (ca_ref[...], sa_ref[...]))
                else:
                    y = _rope_b(y, (cb_ref[...], sb1_ref[...], sb2_ref[...]))
            if scale != 1.0:
                y = y * scale
            o_ref[:, c * LANES:(c + 1) * LANES] = y.astype(BF16)

    def in_range(lo, hi):
        return (j >= lo // IN_TN) & (j < hi // IN_TN)

    @pl.when(in_range(OFF_KA, OFF_VA))
    def _():
        head_epilogue(0, A_HEAD_DIM, "a", 1.0)

    @pl.when(in_range(OFF_KB, OFF_VB))
    def _():
        head_epilogue(1, B_HEAD_DIM, "b", 1.0)

    @pl.when(in_range(OFF_VA, OFF_KB) | in_range(OFF_VB, OFF_QA))
    def _():
        o_ref[...] = acc_ref[...].astype(BF16)

    if with_q:
        @pl.when(in_range(OFF_QA, OFF_QB))
        def _():
            head_epilogue(2, A_HEAD_DIM, "a", A_HEAD_DIM ** -0.5 * LOG2E)

        @pl.when(in_range(OFF_QB, OFF_GA))
        def _():
            head_epilogue(3, B_HEAD_DIM, "b", B_HEAD_DIM ** -0.5 * LOG2E)

        @pl.when(j >= OFF_GA // IN_TN)
        def _():
            o_ref[...] = _sigmoid(acc_ref[...]).astype(BF16)


def _in_proj(x2d, mod, mod_row_of_block, g1, w_bf16, gains, tabs, *, tm, seq_blocks, use_rope, with_q):
    rows, d = x2d.shape
    ncols = w_bf16.shape[1]
    kern = functools.partial(_in_proj_kernel, use_rope=use_rope, with_q=with_q)
    tab_spec = pl.BlockSpec((tm, LANES), lambda i, j: (i % seq_blocks, 0))
    return pl.pallas_call(
        kern,
        grid=(rows // tm, ncols // IN_TN),
        in_specs=[
            pl.BlockSpec((tm, d), lambda i, j: (i, 0)),
            pl.BlockSpec((1, 1, d), lambda i, j: (mod_row_of_block(i) * ADA_CHUNKS + 0, 0, 0)),
            pl.BlockSpec((1, 1, d), lambda i, j: (mod_row_of_block(i) * ADA_CHUNKS + 1, 0, 0)),
            pl.BlockSpec((1, d), lambda i, j: (0, 0)),
            pl.BlockSpec((d, IN_TN), lambda i, j: (0, j)),
            pl.BlockSpec((8, LANES), lambda i, j: (0, 0)),
            tab_spec, tab_spec, tab_spec, tab_spec, tab_spec,
        ],
        out_specs=pl.BlockSpec((tm, IN_TN), lambda i, j: (i, j)),
        out_shape=jax.ShapeDtypeStruct((rows, ncols), BF16),
        scratch_shapes=[pltpu.VMEM((tm, d), BF16), pltpu.VMEM((tm, IN_TN), F32)],
        compiler_params=pltpu.CompilerParams(
            dimension_semantics=("parallel", "arbitrary"), vmem_limit_bytes=VMEM_LIMIT),
        name="in_proj_q" if with_q else "in_proj_ctx",
    )(x2d, mod, mod, g1, w_bf16, gains, *tabs)


def _attn_kernel(*refs, group, tq, kc, diff):
    if diff:
        (q_ref, kctx_ref, vctx_ref, klat_ref, vlat_ref, lam_ref, subln_ref,
         o_ref, qs_ref, m_ref, l_ref, acc_ref) = refs
    else:
        (q_ref, kctx_ref, vctx_ref, klat_ref, vlat_ref,
         o_ref, qs_ref, m_ref, l_ref, acc_ref) = refs

    if diff:
        q = q_ref[0]
        lane = lax.broadcasted_iota(jnp.int32, q.shape, 1)
        zero = jnp.zeros_like(q)
        qs_ref[0:tq, :] = jnp.where(lane < B_HEAD_DIM, q, zero)
        qs_ref[tq:2 * tq, :] = jnp.where(lane >= B_HEAD_DIM, q, zero)
    else:
        for g in range(group):
            qs_ref[g * tq:(g + 1) * tq, :] = q_ref[0, :, g * LANES:(g + 1) * LANES]

    nt = (((1,), (1,)), ((), ()))
    rows = group * tq

    s = lax.dot_general(qs_ref[...], kctx_ref[0], nt, preferred_element_type=F32)
    m = jnp.max(s, axis=-1, keepdims=True)
    p = jnp.exp2(s - m)
    m_ref[...] = jnp.broadcast_to(m, (rows, LANES))
    l_ref[...] = jnp.broadcast_to(jnp.sum(p, axis=-1, keepdims=True), (rows, LANES))
    acc_ref[...] = jnp.dot(p.astype(BF16), vctx_ref[0], preferred_element_type=F32)

    def body(c, carry):
        start = pl.multiple_of(c * kc, kc)
        k = klat_ref[0, pl.ds(start, kc), :]
        v = vlat_ref[0, pl.ds(start, kc), :]
        s = lax.dot_general(qs_ref[...], k, nt, preferred_element_type=F32)
        m_old = m_ref[...]
        m_new = jnp.maximum(m_old, jnp.max(s, axis=-1, keepdims=True))
        alpha = jnp.exp2(m_old - m_new)
        p = jnp.exp2(s - m_new[:, 0:1])
        l_ref[...] = alpha * l_ref[...] + jnp.sum(p, axis=-1, keepdims=True)
        acc_ref[...] = alpha * acc_ref[...] + jnp.dot(p.astype(BF16), v, preferred_element_type=F32)
        m_ref[...] = m_new
        return carry

    lax.fori_loop(0, klat_ref.shape[1] // kc, body, 0)

    o = acc_ref[...] / l_ref[...]
    if diff:
        lam = (jnp.exp(jnp.sum(lam_ref[0:1, :] * lam_ref[1:2, :], axis=-1, keepdims=True))
               - jnp.exp(jnp.sum(lam_ref[2:3, :] * lam_ref[3:4, :], axis=-1, keepdims=True))
               + LAM_INIT)
        od = o[0:tq] - lam * o[tq:2 * tq]
        inv = lax.rsqrt(jnp.mean(od * od, axis=-1, keepdims=True) + EPS)
        od = od * inv * subln_ref[...] * (1.0 - LAM_INIT)
        o_ref[0] = od.astype(BF16)
    else:
        for g in range(group):
            o_ref[0, :, g * LANES:(g + 1) * LANES] = o[g * tq:(g + 1) * tq].astype(BF16)


def _attention(p_lat, p_ctx, *, diff, lam_vecs=None, subln=None, tq=256, kc=1024):
    b, s, _ = p_lat.shape
    n_ctx = p_ctx.shape[1]
    if diff:
        kv_heads, group, qw = B_HEADS, 2, LANES
        q_blk, k_blk, v_blk = OFF_QB // LANES, OFF_KB // LANES, OFF_VB // LANES
    else:
        kv_heads, group, qw = A_KV_HEADS, A_GROUP, A_GROUP * LANES
        q_blk, k_blk, v_blk = OFF_QA // qw, OFF_KA // LANES, OFF_VA // LANES
    rows = group * tq
    in_specs = [
        pl.BlockSpec((1, tq, qw), lambda bi, h, qi: (bi, qi, q_blk + h)),
        pl.BlockSpec((1, n_ctx, LANES), lambda bi, h, qi: (bi, 0, k_blk + h)),
        pl.BlockSpec((1, n_ctx, LANES), lambda bi, h, qi: (bi, 0, v_blk + h)),
        pl.BlockSpec((1, s, LANES), lambda bi, h, qi: (bi, 0, k_blk + h)),
        pl.BlockSpec((1, s, LANES), lambda bi, h, qi: (bi, 0, v_blk + h)),
    ]
    args = [p_lat, p_ctx, p_ctx, p_lat, p_lat]
    if diff:
        in_specs += [pl.BlockSpec((4, LANES), lambda bi, h, qi: (0, 0)),
                     pl.BlockSpec((1, LANES), lambda bi, h, qi: (0, 0))]
        args += [lam_vecs, subln]
    kern = functools.partial(_attn_kernel, group=group, tq=tq, kc=kc, diff=diff)
    return pl.pallas_call(
        kern,
        grid=(b, kv_heads, s // tq),
        in_specs=in_specs,
        out_specs=pl.BlockSpec((1, tq, qw), lambda bi, h, qi: (bi, qi, h)),
        out_shape=jax.ShapeDtypeStruct((b, s, kv_heads * qw), BF16),
        scratch_shapes=[pltpu.VMEM((rows, LANES), BF16), pltpu.VMEM((rows, LANES), F32),
                        pltpu.VMEM((rows, LANES), F32), pltpu.VMEM((rows, LANES), F32)],
        compiler_params=pltpu.CompilerParams(
            dimension_semantics=("parallel", "parallel", "arbitrary"), vmem_limit_bytes=VMEM_LIMIT),
        name="attn_b" if diff else "attn_a",
    )(*args)


def _merge_kernel(oa_ref, ob_ref, ga_ref, gb_ref, x_ref, gate_ref, shift_ref, scale_ref, g2_ref,
                  wa_ref, wb_ref, wo_ref, xn_ref, h_ref):
    ta = jnp.dot(oa_ref[...], wa_ref[...], preferred_element_type=F32)
    tb = jnp.dot(ob_ref[...], wb_ref[...], preferred_element_type=F32)
    merged = ga_ref[...].astype(F32) * ta + gb_ref[...].astype(F32) * tb
    u = jnp.dot(merged.astype(BF16), wo_ref[...], preferred_element_type=F32)
    xn = x_ref[...] + gate_ref[0] * u
    xn_ref[...] = xn
    inv = lax.rsqrt(jnp.mean(xn * xn, axis=-1, keepdims=True) + EPS)
    h = xn * inv * g2_ref[...]
    h_ref[...] = (h * (1.0 + scale_ref[0]) + shift_ref[0]).astype(BF16)


def _merge(oa, ob, p_lat2d, x2d, mod, g2, wa, wb, wo, *, tm, blocks_per_batch):
    rows, d = x2d.shape
    const = lambda i: (0, 0)
    mod_spec = lambda chunk: pl.BlockSpec(
        (1, 1, d), lambda i: ((i // blocks_per_batch) * ADA_CHUNKS + chunk, 0, 0))
    return pl.pallas_call(
        _merge_kernel,
        grid=(rows // tm,),
        in_specs=[
            pl.BlockSpec((tm, A_Q), lambda i: (i, 0)),
            pl.BlockSpec((tm, B_V), lambda i: (i, 0)),
            pl.BlockSpec((tm, d), lambda i: (i, OFF_GA // d)),
            pl.BlockSpec((tm, d), lambda i: (i, OFF_GB // d)),
            pl.BlockSpec((tm, d), lambda i: (i, 0)),
            mod_spec(2), mod_spec(3), mod_spec(4),
            pl.BlockSpec((1, d), const),
            pl.BlockSpec((A_Q, d), const, pipeline_mode=pl.Buffered(1)),
            pl.BlockSpec((B_V, d), const, pipeline_mode=pl.Buffered(1)),
            pl.BlockSpec((d, d), const, pipeline_mode=pl.Buffered(1)),
        ],
        out_specs=[pl.BlockSpec((tm, d), lambda i: (i, 0)), pl.BlockSpec((tm, d), lambda i: (i, 0))],
        out_shape=[jax.ShapeDtypeStruct((rows, d), F32), jax.ShapeDtypeStruct((rows, d), BF16)],
        compiler_params=pltpu.CompilerParams(
            dimension_semantics=("parallel",), vmem_limit_bytes=VMEM_LIMIT),
        name="merge",
    )(oa, ob, p_lat2d, p_lat2d, x2d, mod, mod, mod, g2, wa, wb, wo)


def _ffn_kernel(h_ref, xn_ref, gate_ref, wg_ref, wu_ref, wd_ref, o_ref, acc_ref):
    f = pl.program_id(1)
    h = h_ref[...]
    a = jnp.dot(h, wg_ref[...], preferred_element_type=F32)
    u = jnp.dot(h, wu_ref[...], preferred_element_type=F32)
    hf = (_silu(a) * u).astype(BF16)
    part = jnp.dot(hf, wd_ref[...], preferred_element_type=F32)

    @pl.when(f == 0)
    def _():
        acc_ref[...] = part

    @pl.when(f > 0)
    def _():
        acc_ref[...] += part

    @pl.when(f == pl.num_programs(1) - 1)
    def _():
        o_ref[...] = xn_ref[...] + gate_ref[0] * acc_ref[...]


def _ffn(h2, xn, mod, wg, wu, wd, *, tm, tf, blocks_per_batch):
    rows, d = xn.shape
    d_ff = wg.shape[1]
    return pl.pallas_call(
        _ffn_kernel,
        grid=(rows // tm, d_ff // tf),
        in_specs=[
            pl.BlockSpec((tm, d), lambda i, f: (i, 0)),
            pl.BlockSpec((tm, d), lambda i, f: (i, 0)),
            pl.BlockSpec((1, 1, d), lambda i, f: ((i // blocks_per_batch) * ADA_CHUNKS + 5, 0, 0)),
            pl.BlockSpec((d, tf), lambda i, f: (0, f)),
            pl.BlockSpec((d, tf), lambda i, f: (0, f)),
            pl.BlockSpec((tf, d), lambda i, f: (f, 0)),
        ],
        out_specs=pl.BlockSpec((tm, d), lambda i, f: (i, 0)),
        out_shape=jax.ShapeDtypeStruct((rows, d), F32),
        scratch_shapes=[pltpu.VMEM((tm, d), F32)],
        compiler_params=pltpu.CompilerParams(
            dimension_semantics=("parallel", "arbitrary"), vmem_limit_bytes=VMEM_LIMIT),
        name="ffn",
    )(h2, xn, mod, wg, wu, wd)


def _rope_tables(seq):
    rows = seq // GRID_W
    row = jnp.repeat(jnp.arange(rows, dtype=F32), GRID_W)
    col = jnp.tile(jnp.arange(GRID_W, dtype=F32), rows)

    def angles(head_dim):
        n_freq = head_dim // 4
        freqs = ROPE_THETA ** (-jnp.arange(n_freq, dtype=F32) / n_freq)
        ang = jnp.concatenate([row[:, None] * freqs, col[:, None] * freqs], axis=-1)
        return jnp.cos(ang), jnp.sin(ang)

    cos_a, sin_a = angles(A_HEAD_DIM)
    cos_b, sin_b = angles(B_HEAD_DIM)
    zb = jnp.zeros_like(sin_b)
    return (
        jnp.concatenate([cos_a, cos_a], axis=-1),
        jnp.concatenate([-sin_a, sin_a], axis=-1),
        jnp.concatenate([cos_b] * 4, axis=-1),
        jnp.concatenate([-sin_b, zb, -sin_b, zb], axis=-1),
        jnp.concatenate([zb, sin_b, zb, sin_b], axis=-1),
    )


def kernel(x, c, ctx, c_ctx, w_ada, b_ada, norm1_g, w_in, q_norm_a, k_norm_a, q_norm_b, k_norm_b,
           lam_q1, lam_k1, lam_q2, lam_k2, subln_g, w_br_a, w_br_b, w_out, norm2_g,
           w_ff_gate, w_ff_up, w_ff_down):
    batch, seq, d = x.shape
    n_ctx = ctx.shape[1]
    depth = w_ada.shape[0]
    assert depth == 1 and d == D_MODEL and w_in.shape[2] == IN_WIDTH

    cond = jnp.zeros((8, d), F32).at[:batch].set(c).at[batch].set(c_ctx)
    mod = _adaln(cond, w_ada[0], b_ada[0]).reshape(8 * ADA_CHUNKS, 1, d)

    w_in_b = w_in[0].astype(BF16)
    gains = jnp.zeros((8, LANES), F32)
    gains = gains.at[0].set(k_norm_a[0]).at[1].set(jnp.tile(k_norm_b[0], 2))
    gains = gains.at[2].set(q_norm_a[0]).at[3].set(jnp.tile(q_norm_b[0], 2))
    g1 = norm1_g[0].reshape(1, d)
    tabs = _rope_tables(seq)

    tm_in = 1024
    p_lat = _in_proj(x.reshape(batch * seq, d), mod, lambda i: i // (seq // tm_in), g1, w_in_b, gains, tabs,
                     tm=tm_in, seq_blocks=seq // tm_in, use_rope=True, with_q=True)
    tm_ctx = n_ctx
    ctx_tabs = tuple(t[:tm_ctx] for t in tabs)
    p_ctx = _in_proj(ctx.reshape(batch * n_ctx, d), mod, lambda i: batch, g1, w_in_b[:, :KV_WIDTH], gains,
                     ctx_tabs, tm=tm_ctx, seq_blocks=1, use_rope=False, with_q=False)

    p_lat3 = p_lat.reshape(batch, seq, IN_WIDTH)
    p_ctx3 = p_ctx.reshape(batch, n_ctx, KV_WIDTH)
    oa = _attention(p_lat3, p_ctx3, diff=False)
    lam_vecs = jnp.zeros((4, LANES), F32)
    lam_vecs = lam_vecs.at[0, :B_HEAD_DIM].set(lam_q1[0]).at[1, :B_HEAD_DIM].set(lam_k1[0])
    lam_vecs = lam_vecs.at[2, :B_HEAD_DIM].set(lam_q2[0]).at[3, :B_HEAD_DIM].set(lam_k2[0])
    ob = _attention(p_lat3, p_ctx3, diff=True, lam_vecs=lam_vecs, subln=subln_g[0].reshape(1, B_V_DIM))

    tm_merge = 256
    xn, h2 = _merge(oa.reshape(batch * seq, A_Q), ob.reshape(batch * seq, B_V), p_lat, x.reshape(batch * seq, d),
                    mod, norm2_g[0].reshape(1, d), w_br_a[0].astype(BF16), w_br_b[0].astype(BF16),
                    w_out[0].astype(BF16), tm=tm_merge, blocks_per_batch=seq // tm_merge)

    tm_ffn = 512
    out = _ffn(h2, xn, mod, w_ff_gate[0].astype(BF16), w_ff_up[0].astype(BF16), w_ff_down[0].astype(BF16),
               tm=tm_ffn, tf=512, blocks_per_batch=seq // tm_ffn)
    return out.reshape(batch, seq, d)
```
